```python
import jax
import jax.numpy as jnp
from jax import lax
import numpy as np

D_MODEL = 1024
BATCH = 8
SEQ = 2048
DEPTH = 1
DEC_BATCH = 32
DEC_SEQ = 4
PAST_LEN = 8192
PAGE_SIZE = 128

CONV_CH = D_MODEL // 2
CONV_K = 3
N_HEADS = 8
N_KV_HEADS = 2
GROUP = N_HEADS // N_KV_HEADS
HEAD_DIM = 64
CMP_STRIDE = 16
CMP_LEN = 2 * CMP_STRIDE
CMP_HIDDEN = 2 * HEAD_DIM
SEL_BLOCK = 64
SEL_TOPN = 16
WINDOW = 512
BAND_BLOCK = 128
Q_BLOCK = 128
ROPE_THETA = 10000.0
PEER_HEADS = 8
PEER_NKEYS = 128
PEER_EXPERTS = PEER_NKEYS * PEER_NKEYS
PEER_QDIM = 256
PEER_HALF = PEER_QDIM // 2
PEER_TOPK = 16
PEER_BLOCK = 128

RMS_EPS = 1e-6
NEG = -1e30
FORCE_BONUS = 1e4
Q_W = N_HEADS * HEAD_DIM
KV_W = N_KV_HEADS * HEAD_DIM
IN_SIZES = (CONV_CH, CONV_CH, CONV_CH, Q_W, KV_W, KV_W, KV_W, KV_W, KV_W, KV_W, 3 * N_HEADS, 2 * D_MODEL)
IN_WIDTH = sum(IN_SIZES)

kernel_name = 'hybrid_shortconv_nsa_peer_step'


def rmsnorm(x, g):
    xf = x.astype(jnp.float32)
    r = lax.rsqrt(jnp.mean(xf * xf, axis=-1, keepdims=True) + RMS_EPS)
    return (xf * r * g.astype(jnp.float32)).astype(x.dtype)


def rope(x, pos):
    half = HEAD_DIM // 2
    inv = jnp.power(ROPE_THETA, -jnp.arange(half, dtype=jnp.float32) * 2.0 / HEAD_DIM)
    ang = pos.astype(jnp.float32)[:, None] * inv[None, :]
    shape = (1, pos.shape[0]) + (1,) * (x.ndim - 3) + (half,)
    cos = jnp.cos(ang).reshape(shape)
    sin = jnp.sin(ang).reshape(shape)
    xf = x.astype(jnp.float32)
    x1, x2 = xf[..., :half], xf[..., half:]
    return jnp.concatenate([x1 * cos - x2 * sin, x2 * cos + x1 * sin], axis=-1).astype(x.dtype)


def attend(q, k, v, mask):
    s = jnp.einsum('...qgrd,...kgd->...qgrk', q, k).astype(jnp.float32) * (HEAD_DIM ** -0.5)
    p = jax.nn.softmax(jnp.where(mask, s, NEG), axis=-1)
    return jnp.einsum('...qgrk,...kgd->...qgrd', p.astype(v.dtype), v)


def gather_pages(pool, page_table):
    g = pool[page_table]
    return g.reshape((g.shape[0], g.shape[1] * g.shape[2]) + pool.shape[2:])


def compress(raw, pos_emb, w1, b1, w2):
    b, L = raw.shape[:2]
    n_ch = L // CMP_STRIDE
    ch = raw[:, :n_ch * CMP_STRIDE].reshape(b, n_ch, CMP_STRIDE, N_KV_HEADS, HEAD_DIM)
    first = jnp.einsum('bcsgd,sdf->bcgf', ch, w1[:CMP_STRIDE])
    second = jnp.einsum('bcsgd,sdf->bcgf', ch, w1[CMP_STRIDE:])
    bias = jnp.einsum('sd,sdf->f', pos_emb, w1) + b1
    hid = jax.nn.gelu(first[:, :-1] + second[:, 1:] + bias, approximate=False)
    return jnp.einsum('bngf,fd->bngd', hid, w2)


def to_blocks(k):
    b, L = k.shape[:2]
    n_sel = -(-L // SEL_BLOCK)
    k = jnp.pad(k, ((0, 0), (0, n_sel * SEL_BLOCK - L), (0, 0), (0, 0)))
    return k.reshape(b, n_sel, SEL_BLOCK, N_KV_HEADS, HEAD_DIM).transpose(0, 3, 1, 2, 4)


def cmp_to_sel_weights(n_cmp, n_sel):
    c0 = jnp.arange(n_cmp) * CMP_STRIDE
    c1 = c0 + CMP_LEN
    s0 = jnp.arange(n_sel) * SEL_BLOCK
    s1 = s0 + SEL_BLOCK
    ov = jnp.clip(jnp.minimum(c1[:, None], s1[None, :]) - jnp.maximum(c0[:, None], s0[None, :]), 0)
    return ov.astype(jnp.float32) / CMP_STRIDE


def sparse_branches(qn, qr, start, k_cmp, v_cmp, ks_blk, vs_blk):
    b, T = qn.shape[:2]
    qb = min(Q_BLOCK, T)
    nqb = -(-T // qb)
    pad = nqb * qb - T
    padw = ((0, 0), (0, pad), (0, 0), (0, 0), (0, 0))
    qn_b = jnp.pad(qn, padw).reshape(b, nqb, qb, N_KV_HEADS, GROUP, HEAD_DIM).swapaxes(0, 1)
    qr_b = jnp.pad(qr, padw).reshape(b, nqb, qb, N_KV_HEADS, GROUP, HEAD_DIM).swapaxes(0, 1)
    t_b = (start + jnp.arange(nqb * qb, dtype=jnp.int32)).reshape(nqb, qb)
    n_cmp = k_cmp.shape[1]
    n_sel = ks_blk.shape[2]
    k_sel = min(SEL_TOPN, n_sel)
    ov = cmp_to_sel_weights(n_cmp, n_sel)
    cmp_end = jnp.arange(n_cmp, dtype=jnp.int32) * CMP_STRIDE + CMP_LEN - 1
    jsel = jnp.arange(n_sel, dtype=jnp.int32)
    bi = jnp.arange(b)[:, None, None, None]
    gi = jnp.arange(N_KV_HEADS)[None, None, :, None]
    scale = HEAD_DIM ** -0.5

    def one(args):
        q_n, q_r, t = args
        s_c = jnp.einsum('bqgrd,bngd->bqgrn', q_n, k_cmp).astype(jnp.float32) * scale
        vm = (cmp_end[None, :] <= t[:, None])[None, :, None, None, :]
        p_c = jnp.where(vm, jax.nn.softmax(jnp.where(vm, s_c, NEG), axis=-1), 0.0)
        o_c = jnp.einsum('bqgrn,bngd->bqgrd', p_c.astype(v_cmp.dtype), v_cmp)
        p_blk = jnp.einsum('bqgrn,nj->bqgj', p_c, ov)
        cur = t // SEL_BLOCK
        forced = (jsel[None, :] == 0) | (jsel[None, :] == cur[:, None]) | (jsel[None, :] == cur[:, None] - 1)
        future = jsel[None, :] * SEL_BLOCK > t[:, None]
        score = jnp.where(future[None, :, None, :], NEG, p_blk + FORCE_BONUS * forced[None, :, None, :].astype(jnp.float32))
        _, idx = lax.top_k(score, k_sel)
        k_g = ks_blk[bi, gi, idx]
        v_g = vs_blk[bi, gi, idx]
        s_s = jnp.einsum('bqgrd,bqgkld->bqgrkl', q_r, k_g).astype(jnp.float32) * scale
        kpos = idx[..., None] * SEL_BLOCK + jnp.arange(SEL_BLOCK, dtype=jnp.int32)
        valid = (kpos <= t[None, :, None, None, None])[:, :, :, None]
        s_s = jnp.where(valid, s_s, NEG)
        sh = s_s.shape
        p_s = jax.nn.softmax(s_s.reshape(sh[:4] + (sh[4] * sh[5],)), axis=-1).reshape(sh)
        o_s = jnp.einsum('bqgrkl,bqgkld->bqgrd', p_s.astype(v_g.dtype), v_g)
        return o_c.astype(q_n.dtype), o_s.astype(q_n.dtype)

    o_c, o_s = lax.map(one, (qn_b, qr_b, t_b))
    shp = (b, nqb * qb, N_KV_HEADS, GROUP, HEAD_DIM)
    return o_c.swapaxes(0, 1).reshape(shp)[:, :T], o_s.swapaxes(0, 1).reshape(shp)[:, :T]


def window_banded(qr, kw, vw):
    b, T = qr.shape[:2]
    nb = T // BAND_BLOCK
    n_prev = WINDOW // BAND_BLOCK
    padw = ((0, 0), (n_prev * BAND_BLOCK, 0), (0, 0), (0, 0))
    kp = jnp.pad(kw, padw).reshape(b, nb + n_prev, BAND_BLOCK, N_KV_HEADS, HEAD_DIM)
    vp = jnp.pad(vw, padw).reshape(b, nb + n_prev, BAND_BLOCK, N_KV_HEADS, HEAD_DIM)
    kband = jnp.concatenate([kp[:, j:j + nb] for j in range(n_prev + 1)], axis=2)
    vband = jnp.concatenate([vp[:, j:j + nb] for j in range(n_prev + 1)], axis=2)
    qbk = qr.reshape(b, nb, BAND_BLOCK, N_KV_HEADS, GROUP, HEAD_DIM)
    tq = jnp.arange(T, dtype=jnp.int32).reshape(nb, BAND_BLOCK)
    tk = (jnp.arange(nb, dtype=jnp.int32)[:, None] - n_prev) * BAND_BLOCK + jnp.arange((n_prev + 1) * BAND_BLOCK, dtype=jnp.int32)[None, :]
    mask = (tk[:, None, :] <= tq[:, :, None]) & (tk[:, None, :] > tq[:, :, None] - WINDOW) & (tk[:, None, :] >= 0)
    o = attend(qbk, kband, vband, mask[None, :, :, None, None, :])
    return o.reshape(b, T, N_KV_HEADS, GROUP, HEAD_DIM)


def window_cached(qr, kw, vw, kw_prev, vw_prev, start):
    T = qr.shape[1]
    k_all = jnp.concatenate([kw_prev, kw], axis=1)
    v_all = jnp.concatenate([vw_prev, vw], axis=1)
    tk = start - kw_prev.shape[1] + jnp.arange(k_all.shape[1], dtype=jnp.int32)
    tq = start + jnp.arange(T, dtype=jnp.int32)
    mask = (tk[None, :] <= tq[:, None]) & (tk[None, :] > tq[:, None] - WINDOW)
    o = attend(qr, k_all, v_all, mask[None, :, None, None, :])
    keep = min(WINDOW, start + T)
    return o, k_all[:, -keep:], v_all[:, -keep:]


def peer(h, wq, sub_keys, u_tab, v_tab):
    b, T, D = h.shape
    n = b * T
    pb = min(PEER_BLOCK, n)
    nblk = -(-n // pb)
    hf = jnp.pad(h.reshape(n, D), ((0, nblk * pb - n), (0, 0))).reshape(nblk, pb, D)

    def one(hx):
        qq = (hx @ wq).reshape(pb, PEER_HEADS, 2, PEER_HALF)
        s = jnp.einsum('nhcd,hckd->nhck', qq, sub_keys).astype(jnp.float32)
        s1, i1 = lax.top_k(s[:, :, 0], PEER_TOPK)
        s2, i2 = lax.top_k(s[:, :, 1], PEER_TOPK)
        cand = (s1[..., :, None] + s2[..., None, :]).reshape(pb, PEER_HEADS, PEER_TOPK * PEER_TOPK)
        cidx = (i1[..., :, None] * PEER_NKEYS + i2[..., None, :]).reshape(pb, PEER_HEADS, PEER_TOPK * PEER_TOPK)
        top, sel = lax.top_k(cand, PEER_TOPK)
        eidx = jnp.take_along_axis(cidx, sel, axis=-1)
        g = jax.nn.softmax(top, axis=-1)
        a = jax.nn.gelu(jnp.einsum('nd,nhkd->nhk', hx, u_tab[eidx]).astype(jnp.float32), approximate=False)
        w = (g * a).astype(hx.dtype)
        return jnp.einsum('nhk,nhkd->nd', w, v_tab[eidx])

    out = lax.map(one, hf).reshape(nblk * pb, D)[:n]
    return out.reshape(b, T, D)


def block(x, start, past, P):
    b, T, _ = x.shape
    pos = start + jnp.arange(T, dtype=jnp.int32)
    h = rmsnorm(x, P['norm_mix'])
    proj = h @ P['w_in']
    split_pts = np.cumsum(IN_SIZES)[:-1].tolist()
    cb, cc, chh, q, kc, vc, ks, vs, kw, vw, ng, mg = jnp.split(proj, split_pts, axis=-1)

    if past is None:
        conv_prev = jnp.zeros((b, CONV_K - 1, CONV_CH), x.dtype)
    else:
        conv_prev = past['conv']
    conv_in = jnp.concatenate([conv_prev, cc * chh], axis=1)
    cw = P['conv_w']
    conv_out = conv_in[:, 0:T] * cw[0]
    for k in range(1, CONV_K):
        conv_out = conv_out + conv_in[:, k:k + T] * cw[k]
    y_conv = (cb * conv_out) @ P['w_out_conv']
    new_conv = conv_in[:, -(CONV_K - 1):]

    kv_shape = (b, T, N_KV_HEADS, HEAD_DIM)
    qn = rmsnorm(q.reshape(b, T, N_KV_HEADS, GROUP, HEAD_DIM), P['q_norm'])
    qr = rope(qn, pos)
    kc = kc.reshape(kv_shape)
    vc = vc.reshape(kv_shape)
    ks = rope(rmsnorm(ks.reshape(kv_shape), P['k_norm_slc']), pos)
    vs = vs.reshape(kv_shape)
    kw = rope(rmsnorm(kw.reshape(kv_shape), P['k_norm_win']), pos)
    vw = vw.reshape(kv_shape)
    if past is None:
        kc_all, vc_all, ks_all, vs_all = kc, vc, ks, vs
    else:
        kc_all = jnp.concatenate([past['kc'], kc], axis=1)
        vc_all = jnp.concatenate([past['vc'], vc], axis=1)
        ks_all = jnp.concatenate([past['ks'], ks], axis=1)
        vs_all = jnp.concatenate([past['vs'], vs], axis=1)
    k_cmp = rmsnorm(compress(kc_all, P['cmp_pos_k'], P['cmp_w1_k'], P['cmp_b1_k'], P['cmp_w2_k']), P['k_norm_cmp'])
    v_cmp = compress(vc_all, P['cmp_pos_v'], P['cmp_w1_v'], P['cmp_b1_v'], P['cmp_w2_v'])
    o_cmp, o_slc = sparse_branches(qn, qr, start, k_cmp, v_cmp, to_blocks(ks_all), to_blocks(vs_all))
    if past is None:
        o_win = window_banded(qr, kw, vw)
        keep = min(WINDOW, T)
        kw_buf, vw_buf = kw[:, -keep:], vw[:, -keep:]
    else:
        o_win, kw_buf, vw_buf = window_cached(qr, kw, vw, past['kw'], past['vw'], start)
    g = jax.nn.sigmoid(ng.astype(jnp.float32)).reshape(b, T, N_KV_HEADS, GROUP, 3).astype(x.dtype)
    o = g[..., 0:1] * o_cmp + g[..., 1:2] * o_slc + g[..., 2:3] * o_win
    y_nsa = o.reshape(b, T, Q_W) @ P['w_out_nsa']

    ga, gb = jnp.split(jax.nn.sigmoid(mg.astype(jnp.float32)).astype(x.dtype), 2, axis=-1)
    x = x + (ga * y_conv + gb * y_nsa) @ P['w_o']
    x = x + peer(rmsnorm(x, P['norm_ffn']), P['peer_wq'], P['peer_sub_keys'], P['peer_u'], P['peer_v'])
    return x, (kc, vc, ks, vs, kw_buf, vw_buf, new_conv)


def setup_inputs(seed: int = 0) -> dict:
    key = jax.random.key(seed)
    keys = iter(jax.random.split(key, 48))

    def nrm(shape, scale):
        return jax.random.normal(next(keys), shape, jnp.float32) * scale

    def gain(n):
        return 1.0 + nrm((DEPTH, n), 0.1)

    n_pages = PAST_LEN // PAGE_SIZE
    n_pool = (5 * DEC_BATCH * n_pages + 3) // 4
    win_buf = min(WINDOW, PAST_LEN)
    pool_shape = (DEPTH, n_pool, PAGE_SIZE, N_KV_HEADS, HEAD_DIM)
    page_table = jax.random.permutation(next(keys), n_pool)[:DEC_BATCH * n_pages].reshape(DEC_BATCH, n_pages).astype(jnp.int32)
    return {
        'x_prompt': nrm((BATCH, SEQ, D_MODEL), 1.0),
        'x_sample': nrm((DEC_BATCH, DEC_SEQ, D_MODEL), 1.0),
        'cache_cmp_k': nrm(pool_shape, 1.0),
        'cache_cmp_v': nrm(pool_shape, 1.0),
        'cache_slc_k': nrm(pool_shape, 1.0),
        'cache_slc_v': nrm(pool_shape, 1.0),
        'cache_win_k': nrm((DEPTH, DEC_BATCH, win_buf, N_KV_HEADS, HEAD_DIM), 1.0),
        'cache_win_v': nrm((DEPTH, DEC_BATCH, win_buf, N_KV_HEADS, HEAD_DIM), 1.0),
        'state_conv': nrm((DEPTH, DEC_BATCH, CONV_K - 1, CONV_CH), 1.0),
        'page_table': page_table,
        'norm_mix': gain(D_MODEL),
        'w_in': nrm((DEPTH, D_MODEL, IN_WIDTH), D_MODEL ** -0.5),
        'conv_w': nrm((DEPTH, CONV_K, CONV_CH), 0.5),
        'cmp_pos_k': nrm((DEPTH, CMP_LEN, HEAD_DIM), 0.1),
        'cmp_w1_k': nrm((DEPTH, CMP_LEN, HEAD_DIM, CMP_HIDDEN), (CMP_LEN * HEAD_DIM) ** -0.5),
        'cmp_b1_k': nrm((DEPTH, CMP_HIDDEN), 0.01),
        'cmp_w2_k': nrm((DEPTH, CMP_HIDDEN, HEAD_DIM), CMP_HIDDEN ** -0.5),
        'cmp_pos_v': nrm((DEPTH, CMP_LEN, HEAD_DIM), 0.1),
        'cmp_w1_v': nrm((DEPTH, CMP_LEN, HEAD_DIM, CMP_HIDDEN), (CMP_LEN * HEAD_DIM) ** -0.5),
        'cmp_b1_v': nrm((DEPTH, CMP_HIDDEN), 0.01),
        'cmp_w2_v': nrm((DEPTH, CMP_HIDDEN, HEAD_DIM), CMP_HIDDEN ** -0.5),
        'q_norm': gain(HEAD_DIM),
        'k_norm_cmp': gain(HEAD_DIM),
        'k_norm_slc': gain(HEAD_DIM),
        'k_norm_win': gain(HEAD_DIM),
        'w_out_conv': nrm((DEPTH, CONV_CH, D_MODEL), CONV_CH ** -0.5),
        'w_out_nsa': nrm((DEPTH, Q_W, D_MODEL), Q_W ** -0.5),
        'w_o': nrm((DEPTH, D_MODEL, D_MODEL), D_MODEL ** -0.5),
        'norm_ffn': gain(D_MODEL),
        'peer_wq': nrm((DEPTH, D_MODEL, PEER_HEADS * PEER_QDIM), D_MODEL ** -0.5),
        'peer_sub_keys': nrm((DEPTH, PEER_HEADS, 2, PEER_NKEYS, PEER_HALF), PEER_HALF ** -0.5),
        'peer_u': nrm((DEPTH, PEER_EXPERTS, D_MODEL), D_MODEL ** -0.5),
        'peer_v': nrm((DEPTH, PEER_EXPERTS, D_MODEL), PEER_HEADS ** -0.5),
    }


def reference(x_prompt, x_sample, cache_cmp_k, cache_cmp_v, cache_slc_k, cache_slc_v, cache_win_k, cache_win_v, state_conv, page_table, norm_mix, w_in, conv_w, cmp_pos_k, cmp_w1_k, cmp_b1_k, cmp_w2_k, cmp_pos_v, cmp_w1_v, cmp_b1_v, cmp_w2_v, q_norm, k_norm_cmp, k_norm_slc, k_norm_win, w_out_conv, w_out_nsa, w_o, norm_ffn, peer_wq, peer_sub_keys, peer_u, peer_v):
    yp, ys = x_prompt, x_sample
    p_states, s_states = [], []
    for layer in range(DEPTH):
        P = {
            'norm_mix': norm_mix[layer], 'w_in': w_in[layer], 'conv_w': conv_w[layer],
            'cmp_pos_k': cmp_pos_k[layer], 'cmp_w1_k': cmp_w1_k[layer], 'cmp_b1_k': cmp_b1_k[layer], 'cmp_w2_k': cmp_w2_k[layer],
            'cmp_pos_v': cmp_pos_v[layer], 'cmp_w1_v': cmp_w1_v[layer], 'cmp_b1_v': cmp_b1_v[layer], 'cmp_w2_v': cmp_w2_v[layer],
            'q_norm': q_norm[layer], 'k_norm_cmp': k_norm_cmp[layer], 'k_norm_slc': k_norm_slc[layer], 'k_norm_win': k_norm_win[layer],
            'w_out_conv': w_out_conv[layer], 'w_out_nsa': w_out_nsa[layer], 'w_o': w_o[layer], 'norm_ffn': norm_ffn[layer],
            'peer_wq': peer_wq[layer], 'peer_sub_keys': peer_sub_keys[layer], 'peer_u': peer_u[layer], 'peer_v': peer_v[layer],
        }
        yp, st_p = block(yp, 0, None, P)
        p_states.append(st_p)
        past = {
            'kc': gather_pages(cache_cmp_k[layer], page_table),
            'vc': gather_pages(cache_cmp_v[layer], page_table),
            'ks': gather_pages(cache_slc_k[layer], page_table),
            'vs': gather_pages(cache_slc_v[layer], page_table),
            'kw': cache_win_k[layer], 'vw': cache_win_v[layer], 'conv': state_conv[layer],
        }
        ys, st_s = block(ys, PAST_LEN, past, P)
        s_states.append(st_s)
    pn = [jnp.stack(f) for f in zip(*p_states)]
    sn = [jnp.stack(f) for f in zip(*s_states)]
    return (yp, ys, pn[0], pn[1], pn[2], pn[3], pn[4], pn[5], pn[6], sn[0], sn[1], sn[2], sn[3], sn[4], sn[5], sn[6])
```

```python
import functools

import numpy as np
import jax
import jax.numpy as jnp
from jax import lax
from jax.experimental import pallas as pl
from jax.experimental.pallas import tpu as pltpu

F32 = jnp.float32
BF16 = jnp.bfloat16
I32 = jnp.int32

D_MODEL = 1024
BATCH = 8
SEQ = 2048
DEC_BATCH = 32
DEC_SEQ = 4
PAST_LEN = 8192
PAGE_SIZE = 128
N_PAGES = PAST_LEN // PAGE_SIZE

CONV_CH = D_MODEL // 2
CONV_K = 3
N_HEADS = 8
N_KV_HEADS = 2
GROUP = N_HEADS // N_KV_HEADS
HEAD_DIM = 64
CMP_STRIDE = 16
CMP_LEN = 2 * CMP_STRIDE
CMP_HIDDEN = 2 * HEAD_DIM
SEL_BLOCK = 64
SEL_TOPN = 16
WINDOW = 512
Q_BLOCK = 128
ROPE_THETA = 10000.0
PEER_HEADS = 8
PEER_NKEYS = 128
PEER_EXPERTS = PEER_NKEYS * PEER_NKEYS
PEER_QDIM = 256
PEER_HALF = PEER_QDIM // 2
PEER_TOPK = 16

RMS_EPS = 1e-6
NEG = -1e30
FORCE_BONUS = 1e4
ATT_SCALE = HEAD_DIM ** -0.5
Q_W = N_HEADS * HEAD_DIM
KV_W = N_KV_HEADS * HEAD_DIM

LANES = 128
VMEM_LIMIT_BYTES = 56 * 1024 * 1024

COL_CONV = 0
COL_Q = 3 * CONV_CH
COL_KV = COL_Q + Q_W
COL_NG = COL_KV + 6 * KV_W
NG_W = 3 * N_HEADS
COL_MG = COL_NG + LANES
IN_W_PAD = COL_MG + 2 * D_MODEL

NT_DIMS = (((1,), (1,)), ((), ()))


def _params(*sem):
    return pltpu.CompilerParams(dimension_semantics=sem, vmem_limit_bytes=VMEM_LIMIT_BYTES)


def _dot(a, b):
    return jnp.dot(a, b, preferred_element_type=F32)


def _dot_nt(a, b):
    return lax.dot_general(a, b, NT_DIMS, preferred_element_type=F32)


def _split_bf16(x):
    hi = x.astype(BF16)
    lo = (x - hi.astype(F32)).astype(BF16)
    return hi, lo


def _sigmoid(x):
    return 1.0 / (1.0 + jnp.exp(-x))


def _gelu(x):
    return 0.5 * x * (1.0 + lax.erf(x * np.float32(np.sqrt(0.5))))


def _head_rms(x, bd, gain):
    hi, lo = _split_bf16(x * x)
    seg = _dot(hi, bd) + _dot(lo, bd)
    r = lax.rsqrt(seg * (1.0 / HEAD_DIM) + RMS_EPS)
    return x * r * gain


def _rope(x, cos, sin_signed, first_half):
    partner = jnp.where(first_half, pltpu.roll(x, LANES - HEAD_DIM // 2, 1), pltpu.roll(x, HEAD_DIM // 2, 1))
    return x * cos + partner * sin_signed


def _masked_softmax(s, mask):
    m = jnp.max(jnp.where(mask, s, NEG), axis=-1, keepdims=True)
    e = jnp.where(mask, jnp.exp(s - m), 0.0)
    d = jnp.sum(e, axis=-1, keepdims=True)
    return e * (1.0 / jnp.maximum(d, 1e-30))


def _inproj_kernel(x_ref, gain_ref, w_ref, cos_ref, sin_ref, bd_ref, qg_ref, ksg_ref, kwg_ref,
                   cb_ref, u_ref, qn_ref, qr_ref, kc_ref, vc_ref, ks_ref, vs_ref, kw_ref, vw_ref,
                   gn_ref, gab_ref):
    x = x_ref[...]
    r = lax.rsqrt(jnp.mean(x * x, axis=-1, keepdims=True) + RMS_EPS)
    hb = (x * r * gain_ref[...]).astype(BF16)

    def proj(lo, hi):
        return _dot(hb, w_ref[:, lo:hi])

    conv = proj(COL_CONV, COL_Q)
    cb_ref[...] = conv[:, :CONV_CH]
    u_ref[...] = conv[:, CONV_CH:2 * CONV_CH] * conv[:, 2 * CONV_CH:]

    tm = x.shape[0]
    cos = cos_ref[...]
    sin_signed = sin_ref[...]
    bd = bd_ref[...]
    lane = lax.broadcasted_iota(I32, (tm, LANES), 1)
    first_half = (lane & (HEAD_DIM - 1)) < HEAD_DIM // 2
    low = lane < HEAD_DIM

    q = proj(COL_Q, COL_KV)
    for c in range(Q_W // LANES):
        qn = _head_rms(q[:, c * LANES:(c + 1) * LANES], bd, qg_ref[...])
        qr = _rope(qn, cos, sin_signed, first_half)
        for src, dst in ((qn, qn_ref), (qr, qr_ref)):
            swapped = pltpu.roll(src, HEAD_DIM, 1)
            if (2 * c) // GROUP == 0:
                dst[2 * c] = jnp.where(low, src, 0.0)
                dst[2 * c + 1] = jnp.where(low, swapped, 0.0)
            else:
                dst[2 * c] = jnp.where(low, 0.0, swapped)
                dst[2 * c + 1] = jnp.where(low, 0.0, src)

    kv = proj(COL_KV, COL_NG)
    kc_ref[...] = kv[:, 0 * KV_W:1 * KV_W]
    vc_ref[...] = kv[:, 1 * KV_W:2 * KV_W]
    ks_ref[...] = _rope(_head_rms(kv[:, 2 * KV_W:3 * KV_W], bd, ksg_ref[...]), cos, sin_signed, first_half)
    vs_ref[...] = kv[:, 3 * KV_W:4 * KV_W]
    kw_ref[...] = _rope(_head_rms(kv[:, 4 * KV_W:5 * KV_W], bd, kwg_ref[...]), cos, sin_signed, first_half)
    vw_ref[...] = kv[:, 5 * KV_W:6 * KV_W]

    gn_ref[...] = _sigmoid(proj(COL_NG, COL_MG))
    gab_ref[...] = _sigmoid(proj(COL_MG, IN_W_PAD))


def _inproj(x, gain, w_pad, cos_t, sin_t, bd, qg, ksg, kwg, *, tm, rope_tiles):
    n = x.shape[0]
    grid = (n // tm,)
    row = lambda i: (i, 0)
    const = lambda i: (0, 0)
    rope_map = lambda i: (i % rope_tiles, 0)
    f = jax.ShapeDtypeStruct
    out_shape = (
        f((n, CONV_CH), F32), f((n, CONV_CH), F32),
        f((N_HEADS, n, LANES), F32), f((N_HEADS, n, LANES), F32),
        f((n, KV_W), F32), f((n, KV_W), F32), f((n, KV_W), F32),
        f((n, KV_W), F32), f((n, KV_W), F32), f((n, KV_W), F32),
        f((n, LANES), F32), f((n, 2 * D_MODEL), F32),
    )
    qspec = pl.BlockSpec((N_HEADS, tm, LANES), lambda i: (0, i, 0))
    kvspec = pl.BlockSpec((tm, KV_W), row)
    return pl.pallas_call(
        _inproj_kernel,
        grid=grid,
        in_specs=[
            pl.BlockSpec((tm, D_MODEL), row),
            pl.BlockSpec((1, D_MODEL), const),
            pl.BlockSpec((D_MODEL, IN_W_PAD), const),
            pl.BlockSpec((tm, LANES), rope_map),
            pl.BlockSpec((tm, LANES), rope_map),
            pl.BlockSpec((LANES, LANES), const),
            pl.BlockSpec((1, LANES), const),
            pl.BlockSpec((1, LANES), const),
            pl.BlockSpec((1, LANES), const),
        ],
        out_specs=(
            pl.BlockSpec((tm, CONV_CH), row), pl.BlockSpec((tm, CONV_CH), row),
            qspec, qspec, kvspec, kvspec, kvspec, kvspec, kvspec, kvspec,
            pl.BlockSpec((tm, LANES), row), pl.BlockSpec((tm, 2 * D_MODEL), row),
        ),
        out_shape=out_shape,
        compiler_params=_params("parallel"),
        name="inproj",
    )(x, gain, w_pad, cos_t, sin_t, bd, qg, ksg, kwg)


def _chunkproj_kernel(n_parts, n_prefetch, *refs):
    refs = refs[n_prefetch:]
    k_parts = refs[:n_parts]
    v_parts = refs[n_parts:2 * n_parts]
    wk_ref, wv_ref, fk_ref, fv_ref = refs[2 * n_parts:]

    def gather(parts):
        rows = [p[...] for p in parts]
        x = rows[0] if len(rows) == 1 else jnp.concatenate(rows, axis=0)
        return x.astype(BF16)

    fk_ref[...] = _dot(gather(k_parts), wk_ref[...])
    fv_ref[...] = _dot(gather(v_parts), wv_ref[...])


CHUNK_W = CMP_STRIDE * KV_W
FS_W = 4 * CMP_HIDDEN


def _chunkproj_prompt(xk, xv, wk, wv, *, tm):
    n = xk.shape[0]
    row = lambda i: (i, 0)
    const = lambda i: (0, 0)
    return pl.pallas_call(
        functools.partial(_chunkproj_kernel, 1, 0),
        grid=(n // tm,),
        in_specs=[pl.BlockSpec((tm, CHUNK_W), row), pl.BlockSpec((tm, CHUNK_W), row),
                  pl.BlockSpec((CHUNK_W, FS_W), const), pl.BlockSpec((CHUNK_W, FS_W), const)],
        out_specs=(pl.BlockSpec((tm, FS_W), row), pl.BlockSpec((tm, FS_W), row)),
        out_shape=(jax.ShapeDtypeStruct((n, FS_W), F32), jax.ShapeDtypeStruct((n, FS_W), F32)),
        compiler_params=_params("parallel"),
        name="chunkproj_prompt",
    )(xk, xv, wk, wv)


CP_PAGES = 16
CHUNKS_PER_PAGE = PAGE_SIZE // CMP_STRIDE


def _chunkproj_paged(page_table, pool_k, pool_v, wk, wv):
    steps = N_PAGES // CP_PAGES
    rows = CP_PAGES * CHUNKS_PER_PAGE

    def page_spec(j):
        return pl.BlockSpec((None, CHUNKS_PER_PAGE, CHUNK_W),
                            lambda b, s, pt: (pt[b, s * CP_PAGES + j], 0, 0))

    const = lambda b, s, pt: (0, 0)
    out_map = lambda b, s, pt: (b * steps + s, 0)
    n_out = DEC_BATCH * steps * rows
    grid_spec = pltpu.PrefetchScalarGridSpec(
        num_scalar_prefetch=1,
        grid=(DEC_BATCH, steps),
        in_specs=[page_spec(j) for j in range(CP_PAGES)] * 2
        + [pl.BlockSpec((CHUNK_W, FS_W), const), pl.BlockSpec((CHUNK_W, FS_W), const)],
        out_specs=(pl.BlockSpec((rows, FS_W), out_map), pl.BlockSpec((rows, FS_W), out_map)),
    )
    return pl.pallas_call(
        functools.partial(_chunkproj_kernel, CP_PAGES, 1),
        grid_spec=grid_spec,
        out_shape=(jax.ShapeDtypeStruct((n_out, FS_W), F32), jax.ShapeDtypeStruct((n_out, FS_W), F32)),
        compiler_params=_params("parallel", "arbitrary"),
        name="chunkproj_paged",
    )(page_table, *([pool_k] * CP_PAGES), *([pool_v] * CP_PAGES), wk, wv)


def _cmp_combine_kernel(fk_ref, fv_ref, posk_ref, posv_ref, w1k_ref, w1v_ref, b1k_ref, b1v_ref,
                        w2k_ref, w2v_ref, bd_ref, kg_ref, kc_ref, vc_ref):
    n_ch = fk_ref.shape[0]

    def one(f_ref, pos_ref, w1_ref, b1_ref, w2_ref):
        fs = f_ref[...]
        pos = jnp.broadcast_to(pos_ref[...], (8, pos_ref.shape[1])).astype(BF16)
        bias = _dot(pos, w1_ref[...])[0:1] + b1_ref[...]
        bias2 = jnp.concatenate([bias, bias], axis=1)
        first = fs[:, :2 * CMP_HIDDEN]
        second_next = pltpu.roll(fs[:, 2 * CMP_HIDDEN:], n_ch - 1, 0)
        hid = _gelu(first + second_next + bias2)
        return _dot(hid.astype(BF16), w2_ref[...])

    kc = one(fk_ref, posk_ref, w1k_ref, b1k_ref, w2k_ref)
    kc_ref[...] = _head_rms(kc, bd_ref[...], kg_ref[...])
    vc_ref[...] = one(fv_ref, posv_ref, w1v_ref, b1v_ref, w2v_ref)


def _cmp_combine(fk, fv, posk, posv, w1k, w1v, b1k, b1v, w2k, w2v, bd, kg, *, n_ch):
    n = fk.shape[0]
    row = lambda i: (i, 0)
    const = lambda i: (0, 0)
    flat = CMP_LEN * HEAD_DIM
    return pl.pallas_call(
        _cmp_combine_kernel,
        grid=(n // n_ch,),
        in_specs=[pl.BlockSpec((n_ch, FS_W), row), pl.BlockSpec((n_ch, FS_W), row),
                  pl.BlockSpec((1, flat), const), pl.BlockSpec((1, flat), const),
                  pl.BlockSpec((flat, CMP_HIDDEN), const), pl.BlockSpec((flat, CMP_HIDDEN), const),
                  pl.BlockSpec((1, CMP_HIDDEN), const), pl.BlockSpec((1, CMP_HIDDEN), const),
                  pl.BlockSpec((2 * CMP_HIDDEN, KV_W), const), pl.BlockSpec((2 * CMP_HIDDEN, KV_W), const),
                  pl.BlockSpec((LANES, LANES), const), pl.BlockSpec((1, LANES), const)],
        out_specs=(pl.BlockSpec((n_ch, KV_W), row), pl.BlockSpec((n_ch, KV_W), row)),
        out_shape=(jax.ShapeDtypeStruct((n, KV_W), F32), jax.ShapeDtypeStruct((n, KV_W), F32)),
        compiler_params=_params("parallel"),
        name="cmp_combine",
    )(fk, fv, posk, posv, w1k, w1v, b1k, b1v, w2k, w2v, bd, kg)


def _select_blocks(p_blk_t, tq_lane, n_sel, score_ref):
    jn, ln = p_blk_t.shape
    j = lax.broadcasted_iota(I32, (jn, ln), 0)
    cur = tq_lane >> (SEL_BLOCK.bit_length() - 1)
    forced = (j == 0) | (j == cur) | (j == cur - 1)
    future = j * SEL_BLOCK > tq_lane
    score = jnp.where(future, NEG, p_blk_t + FORCE_BONUS * forced.astype(F32))
    score = jnp.where(j < n_sel, score, -jnp.inf)
    score_ref[...] = score

    def body(i, rank):
        row = score_ref[pl.ds(i, 1), :]
        ahead = (row > score) | ((row == score) & (i < j))
        return rank + ahead.astype(F32)

    rank = lax.fori_loop(0, n_sel, body, jnp.zeros((jn, ln), F32))
    keep = (rank < float(min(SEL_TOPN, n_sel))) & (j < n_sel)
    return keep.astype(F32)


def _gate_column(gates, col):
    lane = lax.broadcasted_iota(I32, gates.shape, 1)
    return jnp.sum(jnp.where(lane == col, gates, 0.0), axis=-1, keepdims=True)


def _attn_prompt_kernel(qn_ref, qr_ref, kcmp_ref, vcmp_ref, ks_ref, vs_ref, kw_ref, vw_ref,
                        gate_ref, ovt_ref, expand_ref, o_ref, score_ref):
    i = pl.program_id(1)
    g = pl.program_id(2)
    tq = Q_BLOCK
    rows = GROUP * tq
    t0 = i * tq

    qn = qn_ref[...].reshape(rows, LANES).astype(BF16)
    qr = qr_ref[...].reshape(rows, LANES).astype(BF16)

    n_cmp_pad = kcmp_ref.shape[0]
    s_c = (_dot_nt(qn, kcmp_ref[...].astype(BF16)) * ATT_SCALE).reshape(GROUP, tq, n_cmp_pad)
    tq_rows = t0 + lax.broadcasted_iota(I32, (tq, n_cmp_pad), 0)
    cmp_end = lax.broadcasted_iota(I32, (tq, n_cmp_pad), 1) * CMP_STRIDE + (CMP_LEN - 1)
    p_c = _masked_softmax(s_c, (cmp_end <= tq_rows)[None])
    o_c = _dot(p_c.reshape(rows, n_cmp_pad).astype(BF16), vcmp_ref[...].astype(BF16))

    p_sum = p_c[0] + p_c[1] + p_c[2] + p_c[3]
    hi, lo = _split_bf16(p_sum)
    ovt = ovt_ref[...]
    p_blk_t = _dot_nt(ovt, hi) + _dot_nt(ovt, lo)
    n_sel = SEQ // SEL_BLOCK
    tq_lane = t0 + lax.broadcasted_iota(I32, (1, tq), 1)
    keep_t = _select_blocks(p_blk_t, tq_lane, n_sel, score_ref)
    keep = jnp.transpose(keep_t).astype(BF16)
    keep_keys = _dot(keep, expand_ref[...]) > 0.5

    key = lax.broadcasted_iota(I32, (tq, SEQ), 1)
    tq_k = t0 + lax.broadcasted_iota(I32, (tq, SEQ), 0)
    causal = key <= tq_k

    def branch(k_ref, v_ref, mask):
        s = (_dot_nt(qr, k_ref[...].astype(BF16)) * ATT_SCALE).reshape(GROUP, tq, SEQ)
        p = _masked_softmax(s, mask[None])
        return _dot(p.reshape(rows, SEQ).astype(BF16), v_ref[...].astype(BF16))

    o_s = branch(ks_ref, vs_ref, keep_keys & causal)
    o_w = branch(kw_ref, vw_ref, causal & (key > tq_k - WINDOW))

    gates = gate_ref[...]
    lane = lax.broadcasted_iota(I32, (tq, LANES), 1)
    own_half = (lane >> (HEAD_DIM.bit_length() - 1)) == g
    heads = []
    for r in range(GROUP):
        col = jnp.full((tq, 1), 3, I32) * (g * GROUP + r)
        sl = slice(r * tq, (r + 1) * tq)
        o = (_gate_column(gates, col) * o_c[sl] + _gate_column(gates, col + 1) * o_s[sl]
             + _gate_column(gates, col + 2) * o_w[sl])
        heads.append(jnp.where(own_half, o, 0.0))
    is_g0 = g == 0
    for c in range(GROUP // 2):
        even, odd = heads[2 * c], heads[2 * c + 1]
        even = jnp.where(is_g0, even, pltpu.roll(even, HEAD_DIM, 1))
        odd = jnp.where(is_g0, pltpu.roll(odd, HEAD_DIM, 1), odd)
        o_ref[:, c * LANES:(c + 1) * LANES] = even + odd


def _attn_prompt(qn, qr, kcmp, vcmp, ks, vs, kw, vw, gates, ovt, expand):
    nqb = SEQ // Q_BLOCK
    n_ch = SEQ // CMP_STRIDE
    qspec = pl.BlockSpec((GROUP, Q_BLOCK, LANES), lambda b, i, g: (g, b * nqb + i, 0))
    cspec = pl.BlockSpec((n_ch, KV_W), lambda b, i, g: (b, 0))
    kvspec = pl.BlockSpec((SEQ, KV_W), lambda b, i, g: (b, 0))
    const = lambda b, i, g: (0, 0)
    return pl.pallas_call(
        _attn_prompt_kernel,
        grid=(BATCH, nqb, N_KV_HEADS),
        in_specs=[qspec, qspec, cspec, cspec, kvspec, kvspec, kvspec, kvspec,
                  pl.BlockSpec((Q_BLOCK, LANES), lambda b, i, g: (b * nqb + i, 0)),
                  pl.BlockSpec(ovt.shape, const), pl.BlockSpec(expand.shape, const)],
        out_specs=pl.BlockSpec((Q_BLOCK, GROUP * HEAD_DIM), lambda b, i, g: (b * nqb + i, g)),
        out_shape=jax.ShapeDtypeStruct((BATCH * SEQ, Q_W), F32),
        scratch_shapes=[pltpu.VMEM((LANES, Q_BLOCK), F32)],
        compiler_params=_params("parallel", "parallel", "arbitrary"),
        name="attn_prompt",
    )(qn, qr, kcmp, vcmp, ks, vs, kw, vw, gates, ovt, expand)


AD_PAGES = 8
AD_STEPS = N_PAGES // AD_PAGES
AD_KEYS = AD_PAGES * PAGE_SIZE
ROWS_D = N_HEADS * 8
N_SEL_D = -(-(PAST_LEN + DEC_SEQ) // SEL_BLOCK)
SELP_D = 256


def _attn_decode_kernel(pt_ref, qn_ref, qr_ref, kcmp_ref, vcmp_ref, *refs):
    kpages = refs[:AD_PAGES]
    vpages = refs[AD_PAGES:2 * AD_PAGES]
    (ksn_ref, vsn_ref, cwk_ref, cwv_ref, kwn_ref, vwn_ref, gate_ref, ovt_ref, exp_ref, expn_ref,
     gsum_ref, o_ref, keep_ref, m_ref, l_ref, acc_ref, oc_ref, score_ref) = refs[2 * AD_PAGES:]
    del pt_ref
    step = pl.program_id(1)
    t_row = lax.broadcasted_iota(I32, (ROWS_D, 1), 0) & 7
    tq_row = PAST_LEN + t_row

    @pl.when(step == 0)
    def _():
        n_cmp_pad = kcmp_ref.shape[0]
        qn = qn_ref[...].astype(BF16)
        s_c = _dot_nt(qn, kcmp_ref[...].astype(BF16)) * ATT_SCALE
        cmp_end = lax.broadcasted_iota(I32, (ROWS_D, n_cmp_pad), 1) * CMP_STRIDE + (CMP_LEN - 1)
        p_c = _masked_softmax(s_c, cmp_end <= tq_row)
        oc_ref[...] = _dot(p_c.astype(BF16), vcmp_ref[...].astype(BF16))
        hi, lo = _split_bf16(p_c)
        gsum = gsum_ref[...]
        p_sum = _dot(gsum, hi) + _dot(gsum, lo)
        p_sum = jnp.concatenate([p_sum, jnp.zeros((LANES - ROWS_D, n_cmp_pad), F32)], axis=0)
        hi, lo = _split_bf16(p_sum)
        ovt = ovt_ref[...]
        p_blk_t = _dot_nt(ovt, hi) + _dot_nt(ovt, lo)
        tq_lane = PAST_LEN + (lax.broadcasted_iota(I32, (1, LANES), 1) & 7)
        keep_t = _select_blocks(p_blk_t, tq_lane, N_SEL_D, score_ref)
        keep_ref[...] = jnp.transpose(keep_t)[:ROWS_D].astype(BF16)
        m_ref[...] = jnp.full(m_ref.shape, NEG, F32)
        l_ref[...] = jnp.zeros(l_ref.shape, F32)
        acc_ref[...] = jnp.zeros(acc_ref.shape, F32)

    qr = qr_ref[...].astype(BF16)
    keep = keep_ref[...]

    def online(s, mask, v):
        m_old = m_ref[...]
        m_new = jnp.maximum(m_old, jnp.max(jnp.where(mask, s, NEG), axis=-1, keepdims=True))
        alpha = jnp.exp(m_old - m_new)
        p = jnp.where(mask, jnp.exp(s - m_new[:, 0:1]), 0.0)
        l_ref[...] = alpha * l_ref[...] + jnp.sum(p, axis=-1, keepdims=True)
        acc_ref[...] = alpha * acc_ref[...] + _dot(p.astype(BF16), v)
        m_ref[...] = m_new

    k_hist = jnp.concatenate([p[...] for p in kpages], axis=0).astype(BF16)
    v_hist = jnp.concatenate([p[...] for p in vpages], axis=0).astype(BF16)
    s_h = _dot_nt(qr, k_hist) * ATT_SCALE
    kpos = step * AD_KEYS + lax.broadcasted_iota(I32, (ROWS_D, AD_KEYS), 1)
    online(s_h, (_dot(keep, exp_ref[...]) > 0.5) & (kpos <= tq_row), v_hist)

    @pl.when(step == AD_STEPS - 1)
    def _():
        pad = jnp.zeros((LANES - 8, LANES), F32)
        lane = lax.broadcasted_iota(I32, (ROWS_D, LANES), 1)
        k_new = jnp.concatenate([ksn_ref[...], pad], axis=0).astype(BF16)
        v_new = jnp.concatenate([vsn_ref[...], pad], axis=0).astype(BF16)
        s_n = _dot_nt(qr, k_new) * ATT_SCALE
        mask_n = (_dot(keep, expn_ref[...]) > 0.5) & (lane < DEC_SEQ) & (lane <= t_row)
        online(s_n, mask_n, v_new)
        o_s = acc_ref[...] * (1.0 / jnp.maximum(l_ref[...], 1e-30))

        k_w = jnp.concatenate([cwk_ref[...], kwn_ref[...], pad], axis=0).astype(BF16)
        v_w = jnp.concatenate([cwv_ref[...], vwn_ref[...], pad], axis=0).astype(BF16)
        n_w = WINDOW + LANES
        idx = lax.broadcasted_iota(I32, (ROWS_D, n_w), 1)
        mask_w = ((idx < WINDOW) & (idx > t_row)) | ((idx >= WINDOW) & (idx < WINDOW + DEC_SEQ) & (idx - WINDOW <= t_row))
        p_w = _masked_softmax(_dot_nt(qr, k_w) * ATT_SCALE, mask_w)
        o_w = _dot(p_w.astype(BF16), v_w)

        gates = jnp.concatenate([gate_ref[...]] * N_HEADS, axis=0)
        col = (lax.broadcasted_iota(I32, (ROWS_D, 1), 0) >> 3) * 3
        o_ref[...] = (_gate_column(gates, col) * oc_ref[...] + _gate_column(gates, col + 1) * o_s
                      + _gate_column(gates, col + 2) * o_w)


def _attn_decode(page_table, qn, qr, kcmp, vcmp, pool_k, pool_v, ksn, vsn, cwk, cwv, kwn, vwn,
                 gates, ovt, expand, expand_new, gsum):
    n_ch = PAST_LEN // CMP_STRIDE
    req3 = lambda b, s, pt: (b, 0, 0)
    const = lambda b, s, pt: (0, 0)

    def page_spec(j):
        return pl.BlockSpec((None, PAGE_SIZE, KV_W), lambda b, s, pt: (pt[b, s * AD_PAGES + j], 0, 0))

    in_specs = (
        [pl.BlockSpec((None, ROWS_D, LANES), req3), pl.BlockSpec((None, ROWS_D, LANES), req3),
         pl.BlockSpec((n_ch, KV_W), lambda b, s, pt: (b, 0)), pl.BlockSpec((n_ch, KV_W), lambda b, s, pt: (b, 0))]
        + [page_spec(j) for j in range(AD_PAGES)] * 2
        + [pl.BlockSpec((None, 8, KV_W), req3), pl.BlockSpec((None, 8, KV_W), req3),
           pl.BlockSpec((None, WINDOW, KV_W), req3), pl.BlockSpec((None, WINDOW, KV_W), req3),
           pl.BlockSpec((None, 8, KV_W), req3), pl.BlockSpec((None, 8, KV_W), req3),
           pl.BlockSpec((None, 8, LANES), req3),
           pl.BlockSpec(ovt.shape, const),
           pl.BlockSpec((SELP_D, AD_KEYS), lambda b, s, pt: (0, s)),
           pl.BlockSpec(expand_new.shape, const),
           pl.BlockSpec(gsum.shape, const)]
    )
    grid_spec = pltpu.PrefetchScalarGridSpec(
        num_scalar_prefetch=1,
        grid=(DEC_BATCH, AD_STEPS),
        in_specs=in_specs,
        out_specs=pl.BlockSpec((None, ROWS_D, LANES), req3),
        scratch_shapes=[pltpu.VMEM((ROWS_D, SELP_D), BF16), pltpu.VMEM((ROWS_D, 1), F32),
                        pltpu.VMEM((ROWS_D, 1), F32), pltpu.VMEM((ROWS_D, LANES), F32),
                        pltpu.VMEM((ROWS_D, LANES), F32), pltpu.VMEM((SELP_D, LANES), F32)],
    )
    return pl.pallas_call(
        _attn_decode_kernel,
        grid_spec=grid_spec,
        out_shape=jax.ShapeDtypeStruct((DEC_BATCH, ROWS_D, LANES), F32),
        compiler_params=_params("parallel", "arbitrary"),
        name="attn_decode",
    )(page_table, qn, qr, kcmp, vcmp, *([pool_k] * AD_PAGES), *([pool_v] * AD_PAGES),
      ksn, vsn, cwk, cwv, kwn, vwn, gates, ovt, expand, expand_new, gsum)


def _topk_rows(v, k, payload=None):
    rn, ln = v.shape
    row = lax.broadcasted_iota(I32, (rn, ln), 0).astype(F32)
    vals, picks = [], []
    for _ in range(k):
        m = jnp.max(v, axis=0, keepdims=True)
        first = jnp.min(jnp.where(v == m, row, float(rn)), axis=0, keepdims=True)
        hit = row == first
        vals.append(m)
        if payload is None:
            picks.append(first)
        else:
            picks.append(jnp.sum(jnp.where(hit, payload, 0.0), axis=0, keepdims=True))
        v = jnp.where(hit, -jnp.inf, v)
    return jnp.concatenate(vals, axis=0), jnp.concatenate(picks, axis=0)


def _mix_kernel(x_ref, cb_ref, u_ref, u1_ref, u2_ref, o_ref, gab_ref, cw_ref, woc_ref, won_ref, wo_ref,
                gain_ref, wq_ref, keys_ref, x2_ref, h2_ref, eidx_ref, gate_ref, qq_ref, eit_ref, gt_ref):
    cw = cw_ref[...]
    conv = u2_ref[...] * cw[0:1] + u1_ref[...] * cw[1:2] + u_ref[...] * cw[2:3]
    y_conv = _dot((cb_ref[...] * conv).astype(BF16), woc_ref[...])
    y_nsa = _dot(o_ref[...].astype(BF16), won_ref[...])
    gab = gab_ref[...]
    merged = gab[:, :D_MODEL] * y_conv + gab[:, D_MODEL:] * y_nsa
    x2 = x_ref[...] + _dot(merged.astype(BF16), wo_ref[...])
    x2_ref[...] = x2
    r = lax.rsqrt(jnp.mean(x2 * x2, axis=-1, keepdims=True) + RMS_EPS)
    h2 = (x2 * r * gain_ref[...]).astype(BF16)
    h2_ref[...] = h2
    qq_ref[...] = _dot(h2, wq_ref[...]).astype(BF16)

    def per_head(h, carry):
        def scores(c):
            off = pl.multiple_of((2 * h + c) * PEER_HALF, PEER_HALF)
            return _dot_nt(keys_ref[2 * h + c], qq_ref[:, pl.ds(off, PEER_HALF)])

        s1, i1 = _topk_rows(scores(0), PEER_TOPK)
        s2, i2 = _topk_rows(scores(1), PEER_TOPK)
        cand = jnp.concatenate([s1[a:a + 1] + s2 for a in range(PEER_TOPK)], axis=0)
        cidx = jnp.concatenate([i1[a:a + 1] * float(PEER_NKEYS) + i2 for a in range(PEER_TOPK)], axis=0)
        top, eidx = _topk_rows(cand, PEER_TOPK, payload=cidx)
        e = jnp.exp(top - top[0:1])
        gate = e / jnp.sum(e, axis=0, keepdims=True)
        off = pl.multiple_of(h * PEER_TOPK, PEER_TOPK)
        eit_ref[pl.ds(off, PEER_TOPK), :] = eidx
        gt_ref[pl.ds(off, PEER_TOPK), :] = gate
        return carry

    lax.fori_loop(0, PEER_HEADS, per_head, 0)
    eidx_ref[...] = jnp.transpose(eit_ref[...]).astype(I32)
    gate_ref[...] = jnp.transpose(gt_ref[...])


def _mix(x, cb, u, u1, u2, o, gab, cw, woc, won, wo, gain, wq, keys, *, tm):
    n = x.shape[0]
    row = lambda i: (i, 0)
    const = lambda i: (0, 0)
    hk = PEER_HEADS * PEER_TOPK
    f = jax.ShapeDtypeStruct
    return pl.pallas_call(
        _mix_kernel,
        grid=(n // tm,),
        in_specs=[pl.BlockSpec((tm, D_MODEL), row),
                  pl.BlockSpec((tm, CONV_CH), row), pl.BlockSpec((tm, CONV_CH), row),
                  pl.BlockSpec((tm, CONV_CH), row), pl.BlockSpec((tm, CONV_CH), row),
                  pl.BlockSpec((tm, Q_W), row), pl.BlockSpec((tm, 2 * D_MODEL), row),
                  pl.BlockSpec((8, CONV_CH), const),
                  pl.BlockSpec((CONV_CH, D_MODEL), const), pl.BlockSpec((Q_W, D_MODEL), const),
                  pl.BlockSpec((D_MODEL, D_MODEL), const), pl.BlockSpec((1, D_MODEL), const),
                  pl.BlockSpec((D_MODEL, PEER_HEADS * PEER_QDIM), const),
                  pl.BlockSpec((2 * PEER_HEADS, PEER_NKEYS, PEER_HALF), lambda i: (0, 0, 0))],
        out_specs=(pl.BlockSpec((tm, D_MODEL), row), pl.BlockSpec((tm, D_MODEL), row),
                   pl.BlockSpec((tm, hk), row), pl.BlockSpec((tm, hk), row)),
        out_shape=(f((n, D_MODEL), F32), f((n, D_MODEL), BF16), f((n, hk), I32), f((n, hk), F32)),
        scratch_shapes=[pltpu.VMEM((tm, PEER_HEADS * PEER_QDIM), BF16),
                        pltpu.VMEM((hk, tm), F32), pltpu.VMEM((hk, tm), F32)],
        compiler_params=_params("parallel"),
        name="mix_retrieve",
    )(x, cb, u, u1, u2, o, gab, cw, woc, won, wo, gain, wq, keys)


PEER_EC = 1024
PEER_I1_PER_STEP = PEER_EC // PEER_NKEYS


def _peer_kernel(x2_ref, h2_ref, eidx_ref, gate_ref, u_ref, v_ref, out_ref, coef_ref, acc_ref):
    j = pl.program_id(1)
    tm = x2_ref.shape[0]

    @pl.when(j == 0)
    def _():
        sub = lax.broadcasted_iota(I32, (PEER_NKEYS, LANES), 0)

        def per_token(n, carry):
            e = jnp.broadcast_to(eidx_ref[pl.ds(n, 1), :], (PEER_NKEYS, LANES))
            gt = jnp.broadcast_to(gate_ref[pl.ds(n, 1), :], (PEER_NKEYS, LANES))
            a = jnp.where((e >> 7) == sub, gt, 0.0).astype(BF16)
            b = jnp.where((e & (PEER_NKEYS - 1)) == sub, 1.0, 0.0).astype(BF16)
            coef_ref[pl.ds(pl.multiple_of(n * PEER_NKEYS, PEER_NKEYS), PEER_NKEYS), :] = _dot_nt(a, b)
            return carry

        lax.fori_loop(0, tm, per_token, 0)
        acc_ref[...] = jnp.zeros(acc_ref.shape, F32)

    act = _gelu(_dot_nt(h2_ref[...], u_ref[...]))
    coef = jnp.concatenate(
        [coef_ref[pl.ds(j * PEER_I1_PER_STEP + a, tm, stride=PEER_NKEYS), :] for a in range(PEER_I1_PER_STEP)],
        axis=1)
    acc_ref[...] += _dot((coef * act).astype(BF16), v_ref[...])

    @pl.when(j == pl.num_programs(1) - 1)
    def _():
        out_ref[...] = x2_ref[...] + acc_ref[...]


def _peer(x2, h2, eidx, gate, u_tab, v_tab, *, tm):
    n = x2.shape[0]
    hk = PEER_HEADS * PEER_TOPK
    row = lambda i, j: (i, 0)
    chunk = lambda i, j: (j, 0)
    return pl.pallas_call(
        _peer_kernel,
        grid=(n // tm, PEER_EXPERTS // PEER_EC),
        in_specs=[pl.BlockSpec((tm, D_MODEL), row), pl.BlockSpec((tm, D_MODEL), row),
                  pl.BlockSpec((tm, hk), row), pl.BlockSpec((tm, hk), row),
                  pl.BlockSpec((PEER_EC, D_MODEL), chunk), pl.BlockSpec((PEER_EC, D_MODEL), chunk)],
        out_specs=pl.BlockSpec((tm, D_MODEL), row),
        out_shape=jax.ShapeDtypeStruct((n, D_MODEL), F32),
        scratch_shapes=[pltpu.VMEM((tm * PEER_NKEYS, LANES), F32), pltpu.VMEM((tm, D_MODEL), F32)],
        compiler_params=_params("parallel", "arbitrary"),
        name="peer_experts",
    )(x2, h2, eidx, gate, u_tab, v_tab)


def _rope_tables(pos):
    half = HEAD_DIM // 2
    inv = jnp.power(ROPE_THETA, -jnp.arange(half, dtype=F32) * 2.0 / HEAD_DIM)
    ang = pos.astype(F32)[:, None] * inv[None, :]
    cos, sin = jnp.cos(ang), jnp.sin(ang)
    cos_t = jnp.concatenate([cos, cos] * (LANES // HEAD_DIM), axis=1)
    sin_t = jnp.concatenate([-sin, sin] * (LANES // HEAD_DIM), axis=1)
    return cos_t, sin_t


def _overlap_t(n_cmp_pad, n_sel_pad, n_cmp, n_sel):
    c0 = np.arange(n_cmp_pad) * CMP_STRIDE
    c1 = c0 + CMP_LEN
    s0 = np.arange(n_sel_pad) * SEL_BLOCK
    s1 = s0 + SEL_BLOCK
    ov = np.clip(np.minimum(c1[None, :], s1[:, None]) - np.maximum(c0[None, :], s0[:, None]), 0, None)
    ov = ov.astype(np.float32) / CMP_STRIDE
    ov[n_sel:, :] = 0.0
    ov[:, n_cmp:] = 0.0
    return jnp.asarray(ov, BF16)


def _expand(n_sel_pad, n_keys, first_key=0):
    blk = (first_key + np.arange(n_keys)) // SEL_BLOCK
    return jnp.asarray((np.arange(n_sel_pad)[:, None] == blk[None, :]).astype(np.float32), BF16)


def _segment_ones():
    seg = np.arange(LANES) // HEAD_DIM
    return jnp.asarray((seg[:, None] == seg[None, :]).astype(np.float32), BF16)


def _group_sum_matrix():
    r = np.arange(ROWS_D)
    same = ((r[:, None] >> 3) // GROUP == (r[None, :] >> 3) // GROUP) & ((r[:, None] & 7) == (r[None, :] & 7))
    return jnp.asarray(same.astype(np.float32), BF16)


def _two_heads(gain):
    return jnp.concatenate([gain, gain]).reshape(1, LANES)


def _cmp_weights(w1, w2):
    eye = jnp.eye(N_KV_HEADS, dtype=F32)
    w1r = w1.reshape(2, CMP_STRIDE, HEAD_DIM, CMP_HIDDEN)
    w1big = jnp.einsum('psdf,hg->shdpgf', w1r, eye).reshape(CHUNK_W, FS_W)
    w2big = jnp.einsum('fd,hg->hfgd', w2, eye).reshape(N_KV_HEADS * CMP_HIDDEN, KV_W)
    return w1big.astype(BF16), w2big.astype(BF16), w1.reshape(CMP_LEN * HEAD_DIM, CMP_HIDDEN).astype(BF16)


def kernel(x_prompt, x_sample, cache_cmp_k, cache_cmp_v, cache_slc_k, cache_slc_v, cache_win_k, cache_win_v, state_conv, page_table, norm_mix, w_in, conv_w, cmp_pos_k, cmp_w1_k, cmp_b1_k, cmp_w2_k, cmp_pos_v, cmp_w1_v, cmp_b1_v, cmp_w2_v, q_norm, k_norm_cmp, k_norm_slc, k_norm_win, w_out_conv, w_out_nsa, w_o, norm_ffn, peer_wq, peer_sub_keys, peer_u, peer_v):
    layer = 0
    n_p = BATCH * SEQ
    n_s = DEC_BATCH * DEC_SEQ
    n_pool = cache_cmp_k.shape[1]

    w = w_in[layer]
    w_pad = jnp.concatenate([w[:, :COL_NG + NG_W], jnp.zeros((D_MODEL, LANES - NG_W), F32), w[:, COL_NG + NG_W:]],
                            axis=1).astype(BF16)
    gain_mix = norm_mix[layer].reshape(1, D_MODEL)
    gain_ffn = norm_ffn[layer].reshape(1, D_MODEL)
    bd = _segment_ones()
    qg, kcg = _two_heads(q_norm[layer]), _two_heads(k_norm_cmp[layer])
    ksg, kwg = _two_heads(k_norm_slc[layer]), _two_heads(k_norm_win[layer])
    w1k, w2k, w1k_flat = _cmp_weights(cmp_w1_k[layer], cmp_w2_k[layer])
    w1v, w2v, w1v_flat = _cmp_weights(cmp_w1_v[layer], cmp_w2_v[layer])
    posk = cmp_pos_k[layer].reshape(1, CMP_LEN * HEAD_DIM)
    posv = cmp_pos_v[layer].reshape(1, CMP_LEN * HEAD_DIM)
    b1k = cmp_b1_k[layer].reshape(1, CMP_HIDDEN)
    b1v = cmp_b1_v[layer].reshape(1, CMP_HIDDEN)
    cw = jnp.concatenate([conv_w[layer], jnp.zeros((8 - CONV_K, CONV_CH), F32)], axis=0)
    woc = w_out_conv[layer].astype(BF16)
    won = w_out_nsa[layer].astype(BF16)
    wo = w_o[layer].astype(BF16)
    wq = peer_wq[layer].astype(BF16)
    keys = peer_sub_keys[layer].reshape(2 * PEER_HEADS, PEER_NKEYS, PEER_HALF).astype(BF16)
    u_tab = peer_u[layer].astype(BF16)
    v_tab = peer_v[layer].astype(BF16)

    def mix_and_peer(x, cb, u, hist, o, gab, t_len, tm):
        b = x.shape[0] // t_len
        conv_in = jnp.concatenate([hist, u.reshape(b, t_len, CONV_CH)], axis=1)
        u2 = conv_in[:, 0:t_len].reshape(-1, CONV_CH)
        u1 = conv_in[:, 1:t_len + 1].reshape(-1, CONV_CH)
        x2, h2, eidx, gate = _mix(x, cb, u, u1, u2, o, gab, cw, woc, won, wo, gain_ffn, wq, keys, tm=tm)
        y = _peer(x2, h2, eidx, gate, u_tab, v_tab, tm=tm)
        return y, conv_in[:, -(CONV_K - 1):]

    xp = x_prompt.reshape(n_p, D_MODEL)
    cos_p, sin_p = _rope_tables(jnp.arange(SEQ, dtype=I32))
    tm_p = 256
    (cb, u, qn, qr, kc, vc, ks, vs, kw, vw, gn, gab) = _inproj(
        xp, gain_mix, w_pad, cos_p, sin_p, bd, qg, ksg, kwg, tm=tm_p, rope_tiles=SEQ // tm_p)
    n_ch_p = SEQ // CMP_STRIDE
    fk, fv = _chunkproj_prompt(kc.reshape(n_p // CMP_STRIDE, CHUNK_W), vc.reshape(n_p // CMP_STRIDE, CHUNK_W),
                               w1k, w1v, tm=256)
    kcmp, vcmp = _cmp_combine(fk, fv, posk, posv, w1k_flat, w1v_flat, b1k, b1v, w2k, w2v, bd, kcg, n_ch=n_ch_p)
    ovt_p = _overlap_t(n_ch_p, LANES, n_ch_p - 1, SEQ // SEL_BLOCK)
    o_p = _attn_prompt(qn, qr, kcmp, vcmp, ks, vs, kw, vw, gn, ovt_p, _expand(LANES, SEQ))
    y_p, conv_p = mix_and_peer(xp, cb, u, jnp.zeros((BATCH, CONV_K - 1, CONV_CH), F32), o_p, gab, SEQ, 256)

    def kv_out(a, b, t):
        return a.reshape(1, b, t, N_KV_HEADS, HEAD_DIM)

    keep_p = min(WINDOW, SEQ)
    outs_p = (kv_out(kc, BATCH, SEQ), kv_out(vc, BATCH, SEQ), kv_out(ks, BATCH, SEQ), kv_out(vs, BATCH, SEQ),
              kv_out(kw, BATCH, SEQ)[:, :, -keep_p:], kv_out(vw, BATCH, SEQ)[:, :, -keep_p:], conv_p[None])

    xs = x_sample.reshape(n_s, D_MODEL)
    pos_s = jnp.tile(PAST_LEN + jnp.arange(DEC_SEQ, dtype=I32), DEC_BATCH)
    cos_s, sin_s = _rope_tables(pos_s)
    (cb, u, qn, qr, kc, vc, ks, vs, kw, vw, gn, gab) = _inproj(
        xs, gain_mix, w_pad, cos_s, sin_s, bd, qg, ksg, kwg, tm=n_s, rope_tiles=1)
    pool = lambda c: c[layer].reshape(n_pool, CHUNKS_PER_PAGE, CHUNK_W)
    fk, fv = _chunkproj_paged(page_table, pool(cache_cmp_k), pool(cache_cmp_v), w1k, w1v)
    n_ch_s = (PAST_LEN + DEC_SEQ) // CMP_STRIDE
    kcmp, vcmp = _cmp_combine(fk, fv, posk, posv, w1k_flat, w1v_flat, b1k, b1v, w2k, w2v, bd, kcg, n_ch=n_ch_s)

    def rows_per_request(a):
        a = a.reshape(N_HEADS, DEC_BATCH, DEC_SEQ, LANES).transpose(1, 0, 2, 3)
        a = jnp.pad(a, ((0, 0), (0, 0), (0, 8 - DEC_SEQ), (0, 0)))
        return a.reshape(DEC_BATCH, ROWS_D, LANES)

    def new_rows(a):
        return jnp.pad(a.reshape(DEC_BATCH, DEC_SEQ, KV_W), ((0, 0), (0, 8 - DEC_SEQ), (0, 0)))

    pages = lambda c: c[layer].reshape(n_pool, PAGE_SIZE, KV_W)
    cwk = cache_win_k[layer].reshape(DEC_BATCH, WINDOW, KV_W)
    cwv = cache_win_v[layer].reshape(DEC_BATCH, WINDOW, KV_W)
    ovt_s = _overlap_t(n_ch_s, SELP_D, n_ch_s - 1, N_SEL_D)
    o_rows = _attn_decode(page_table, rows_per_request(qn), rows_per_request(qr), kcmp, vcmp,
                          pages(cache_slc_k), pages(cache_slc_v), new_rows(ks), new_rows(vs), cwk, cwv,
                          new_rows(kw), new_rows(vw), new_rows(gn), ovt_s,
                          _expand(SELP_D, PAST_LEN), _expand(SELP_D, LANES, first_key=PAST_LEN), _group_sum_matrix())
    o_rows = o_rows.reshape(DEC_BATCH, N_KV_HEADS, GROUP, 8, N_KV_HEADS, HEAD_DIM)[:, :, :, :DEC_SEQ]
    o_s = jnp.stack([o_rows[:, g, :, :, g] for g in range(N_KV_HEADS)], axis=1)
    o_s = o_s.transpose(0, 3, 1, 2, 4).reshape(n_s, Q_W)
    y_s, conv_s = mix_and_peer(xs, cb, u, state_conv[layer], o_s, gab, DEC_SEQ, n_s)

    keep_s = min(WINDOW, PAST_LEN + DEC_SEQ)
    win_k = jnp.concatenate([cwk, kw.reshape(DEC_BATCH, DEC_SEQ, KV_W)], axis=1)[:, -keep_s:]
    win_v = jnp.concatenate([cwv, vw.reshape(DEC_BATCH, DEC_SEQ, KV_W)], axis=1)[:, -keep_s:]
    outs_s = (kv_out(kc, DEC_BATCH, DEC_SEQ), kv_out(vc, DEC_BATCH, DEC_SEQ), kv_out(ks, DEC_BATCH, DEC_SEQ),
              kv_out(vs, DEC_BATCH, DEC_SEQ), kv_out(win_k, DEC_BATCH, keep_s), kv_out(win_v, DEC_BATCH, keep_s),
              conv_s[None])

    return (y_p.reshape(BATCH, SEQ, D_MODEL), y_s.reshape(DEC_BATCH, DEC_SEQ, D_MODEL)) + outs_p + outs_s
```
